```python
import math
import jax, jax.numpy as jnp
from jax import lax
import numpy as np

D_MODEL = 1024
BATCH = 4
SEQ = 8192
DEPTH = 1

NORM_EPS = 1e-6
CHUNK = 128
N_BRANCHES = 2
GMLP_WIDTH = D_MODEL
GMLP_GROUPS = 8
GMLP_GROUP_DIM = GMLP_WIDTH // GMLP_GROUPS
SSM_EXPAND = 2
D_INNER = SSM_EXPAND * D_MODEL
HEAD_DIM = 64
N_SSM_HEADS = D_INNER // HEAD_DIM
N_SSM_GROUPS = 8
HEADS_PER_GROUP = N_SSM_HEADS // N_SSM_GROUPS
D_STATE = 128
CONV_WIDTH = 4
CONV_DIM = D_INNER + 2 * N_SSM_GROUPS * D_STATE
SSM_NORM_GROUP = D_INNER // N_SSM_GROUPS
D_FF = 4 * D_MODEL
IN_PROJ_DIM = 2 * GMLP_WIDTH + D_INNER + CONV_DIM + N_SSM_HEADS + N_BRANCHES * D_MODEL
_SPLITS = tuple(np.cumsum([2 * GMLP_WIDTH, D_INNER, CONV_DIM, N_SSM_HEADS]).tolist())

kernel_name = "hybrid_gmlp_ssd_gated_block"


def rms_norm(x, g, eps=NORM_EPS):
    xf = x.astype(jnp.float32)
    out = xf * lax.rsqrt(jnp.mean(xf * xf, axis=-1, keepdims=True) + eps)
    return out.astype(x.dtype) * g


def layer_norm(x, g, b, eps=NORM_EPS):
    xf = x.astype(jnp.float32)
    mu = jnp.mean(xf, axis=-1, keepdims=True)
    var = jnp.mean(jnp.square(xf - mu), axis=-1, keepdims=True)
    out = (xf - mu) * lax.rsqrt(var + eps)
    return out.astype(x.dtype) * g + b


def gmlp_spatial_gating(uv, v_g, v_b, w_spatial, b_spatial):
    bsz, seqlen, _ = uv.shape
    nc = seqlen // CHUNK
    z = jax.nn.gelu(uv, approximate=False)
    u, v = jnp.split(z, 2, axis=-1)
    v = layer_norm(v, v_g, v_b)
    v = v.reshape(bsz, nc, CHUNK, GMLP_GROUPS, GMLP_GROUP_DIM)
    causal = jnp.tril(jnp.ones((CHUNK, CHUNK), dtype=bool))
    w = jnp.where(causal[None], w_spatial, jnp.zeros_like(w_spatial))
    s = jnp.einsum("gij,bcjgd->bcigd", w, v) + b_spatial.T[None, None, :, :, None]
    return u * s.reshape(bsz, seqlen, GMLP_WIDTH)


def causal_depthwise_conv(x, w, b):
    y = lax.conv_general_dilated(
        x, w, window_strides=(1,), padding=[(CONV_WIDTH - 1, 0)],
        dimension_numbers=("NWC", "WIO", "NWC"), feature_group_count=x.shape[-1])
    return y + b


def ssd_chunked(xh, dt, a, bm, cm):
    bsz, seqlen = xh.shape[:2]
    nc = seqlen // CHUNK
    xc = xh.reshape(bsz, nc, CHUNK, N_SSM_GROUPS, HEADS_PER_GROUP, HEAD_DIM)
    dtc = dt.reshape(bsz, nc, CHUNK, N_SSM_GROUPS, HEADS_PER_GROUP)
    bc = bm.reshape(bsz, nc, CHUNK, N_SSM_GROUPS, D_STATE)
    cc = cm.reshape(bsz, nc, CHUNK, N_SSM_GROUPS, D_STATE)
    xdt = xc * dtc[..., None]
    da = (dtc * a).astype(jnp.float32).transpose(0, 3, 4, 1, 2)
    cs = jnp.cumsum(da, axis=-1)
    causal = jnp.tril(jnp.ones((CHUNK, CHUNK), dtype=bool))
    seg = cs[..., :, None] - cs[..., None, :]
    lmat = jnp.exp(jnp.where(causal, seg, -jnp.inf))
    cb = jnp.einsum("bclgn,bcsgn->bgcls", cc, bc)
    m = cb[:, :, None] * lmat
    y_diag = jnp.einsum("bgrcls,bcsgrp->bclgrp", m, xdt)
    decay_states = jnp.exp(cs[..., -1:] - cs)
    states = jnp.einsum("bcsgn,bgrcs,bcsgrp->bcgrpn", bc, decay_states, xdt)
    chunk_decay = jnp.exp(cs[..., -1])

    def step(h, inp):
        st, dec = inp
        return h * dec[..., None, None] + st, h

    h0 = jnp.zeros_like(states[:, 0])
    _, prev = lax.scan(step, h0, (jnp.moveaxis(states, 1, 0), jnp.moveaxis(chunk_decay, -1, 0)))
    prev = jnp.moveaxis(prev, 0, 1)
    y_off = jnp.einsum("bclgn,bcgrpn,bgrcl->bclgrp", cc, prev, jnp.exp(cs))
    y = (y_diag + y_off).reshape(bsz, seqlen, N_SSM_GROUPS, HEADS_PER_GROUP, HEAD_DIM)
    return y.astype(xh.dtype)


def mamba2_branch(z, xbc, dt_raw, conv_w, conv_b, dt_bias, a_log, d_skip, ssm_norm_g):
    bsz, seqlen, _ = z.shape
    xbc = jax.nn.silu(causal_depthwise_conv(xbc, conv_w, conv_b))
    xs, bm, cm = jnp.split(xbc, [D_INNER, D_INNER + N_SSM_GROUPS * D_STATE], axis=-1)
    xh = xs.reshape(bsz, seqlen, N_SSM_GROUPS, HEADS_PER_GROUP, HEAD_DIM)
    bm = bm.reshape(bsz, seqlen, N_SSM_GROUPS, D_STATE)
    cm = cm.reshape(bsz, seqlen, N_SSM_GROUPS, D_STATE)
    dt = jax.nn.softplus(dt_raw + dt_bias).reshape(bsz, seqlen, N_SSM_GROUPS, HEADS_PER_GROUP)
    a = -jnp.exp(a_log.astype(jnp.float32)).reshape(N_SSM_GROUPS, HEADS_PER_GROUP)
    y = ssd_chunked(xh, dt, a, bm, cm)
    y = y + d_skip.reshape(N_SSM_GROUPS, HEADS_PER_GROUP)[:, :, None] * xh
    y = y.reshape(bsz, seqlen, D_INNER)
    yg = (y * jax.nn.silu(z)).reshape(bsz, seqlen, N_SSM_GROUPS, SSM_NORM_GROUP)
    yf = yg.astype(jnp.float32)
    yn = yf * lax.rsqrt(jnp.mean(yf * yf, axis=-1, keepdims=True) + NORM_EPS)
    return yn.reshape(bsz, seqlen, D_INNER).astype(z.dtype) * ssm_norm_g


def setup_inputs(seed: int = 0) -> dict:
    key = jax.random.key(seed)
    ks = jax.random.split(key, 24)
    L = DEPTH

    def nrm(k, shape, scale):
        return jax.random.normal(k, shape, jnp.float32) * scale

    x = nrm(ks[0], (BATCH, SEQ, D_MODEL), 1.0)
    norm_mix_g = 1.0 + nrm(ks[1], (L, D_MODEL), 0.02)
    w_in = nrm(ks[2], (L, D_MODEL, IN_PROJ_DIM), D_MODEL ** -0.5)
    conv_w = nrm(ks[3], (L, CONV_WIDTH, 1, CONV_DIM), CONV_WIDTH ** -0.5)
    conv_b = nrm(ks[4], (L, CONV_DIM), 0.02)
    dt0 = jnp.exp(jax.random.uniform(ks[5], (L, N_SSM_HEADS), jnp.float32,
                                     minval=math.log(1e-3), maxval=math.log(1e-1)))
    dt_bias = dt0 + jnp.log(-jnp.expm1(-dt0))
    a_log = jnp.log(jax.random.uniform(ks[6], (L, N_SSM_HEADS), jnp.float32, minval=1.0, maxval=16.0))
    d_skip = 1.0 + nrm(ks[7], (L, N_SSM_HEADS), 0.02)
    ssm_norm_g = 1.0 + nrm(ks[8], (L, D_INNER), 0.02)
    v_norm_g = 1.0 + nrm(ks[9], (L, GMLP_WIDTH), 0.02)
    v_norm_b = nrm(ks[10], (L, GMLP_WIDTH), 0.02)
    w_spatial = nrm(ks[11], (L, GMLP_GROUPS, CHUNK, CHUNK), CHUNK ** -0.5)
    b_spatial = 1.0 + nrm(ks[12], (L, GMLP_GROUPS, CHUNK), 0.02)
    b_gates = nrm(ks[13], (L, N_BRANCHES * D_MODEL), 0.02)
    w_proj_a = nrm(ks[14], (L, GMLP_WIDTH, D_MODEL), GMLP_WIDTH ** -0.5)
    w_proj_b = nrm(ks[15], (L, D_INNER, D_MODEL), D_INNER ** -0.5)
    w_out = nrm(ks[16], (L, D_MODEL, D_MODEL), D_MODEL ** -0.5)
    norm_mlp_g = 1.0 + nrm(ks[17], (L, D_MODEL), 0.02)
    w_mlp_up = nrm(ks[18], (L, D_MODEL, D_FF), D_MODEL ** -0.5)
    w_mlp_down = nrm(ks[19], (L, D_FF, D_MODEL), D_FF ** -0.5)
    norm_final_g = 1.0 + nrm(ks[20], (D_MODEL,), 0.02)
    return {"x": x, "norm_mix_g": norm_mix_g, "w_in": w_in, "conv_w": conv_w, "conv_b": conv_b,
            "dt_bias": dt_bias, "a_log": a_log, "d_skip": d_skip, "ssm_norm_g": ssm_norm_g,
            "v_norm_g": v_norm_g, "v_norm_b": v_norm_b, "w_spatial": w_spatial, "b_spatial": b_spatial,
            "b_gates": b_gates, "w_proj_a": w_proj_a, "w_proj_b": w_proj_b, "w_out": w_out,
            "norm_mlp_g": norm_mlp_g, "w_mlp_up": w_mlp_up, "w_mlp_down": w_mlp_down,
            "norm_final_g": norm_final_g}


def reference(x, norm_mix_g, w_in, conv_w, conv_b, dt_bias, a_log, d_skip, ssm_norm_g,
              v_norm_g, v_norm_b, w_spatial, b_spatial, b_gates, w_proj_a, w_proj_b, w_out,
              norm_mlp_g, w_mlp_up, w_mlp_down, norm_final_g):
    for i in range(DEPTH):
        h = rms_norm(x, norm_mix_g[i])
        proj = h @ w_in[i]
        uv, z, xbc, dt_raw, gate_logits = jnp.split(proj, _SPLITS, axis=-1)
        y_a = gmlp_spatial_gating(uv, v_norm_g[i], v_norm_b[i], w_spatial[i], b_spatial[i])
        y_b = mamba2_branch(z, xbc, dt_raw, conv_w[i], conv_b[i], dt_bias[i], a_log[i],
                            d_skip[i], ssm_norm_g[i])
        gates = jax.nn.sigmoid(gate_logits + b_gates[i])
        gate_a, gate_b = jnp.split(gates, 2, axis=-1)
        merged = gate_a * (y_a @ w_proj_a[i]) + gate_b * (y_b @ w_proj_b[i])
        x = x + merged @ w_out[i]
        h2 = rms_norm(x, norm_mlp_g[i])
        x = x + jnp.square(jax.nn.relu(h2 @ w_mlp_up[i])) @ w_mlp_down[i]
    return rms_norm(x, norm_final_g)
```

```python
import functools
import math

import jax
import jax.numpy as jnp
from jax import lax
from jax.experimental import pallas as pl
from jax.experimental.pallas import tpu as pltpu

F32 = jnp.float32
BF16 = jnp.bfloat16

D_MODEL = 1024
NORM_EPS = 1e-6
CHUNK = 128
GMLP_WIDTH = D_MODEL
GMLP_GROUPS = 8
GMLP_GROUP_DIM = GMLP_WIDTH // GMLP_GROUPS
D_INNER = 2 * D_MODEL
HEAD_DIM = 64
N_HEADS = D_INNER // HEAD_DIM
N_GROUPS = 8
HEADS_PER_GROUP = N_HEADS // N_GROUPS
D_STATE = 128
CONV_WIDTH = 4
BC_DIM = N_GROUPS * D_STATE
CONV_DIM = D_INNER + 2 * BC_DIM
GROUP_INNER = D_INNER // N_GROUPS
D_FF = 4 * D_MODEL
LANES = 128
DT_PAD = LANES
SQRT_HALF = math.sqrt(0.5)

VMEM_LIMIT = 56 * 1024 * 1024


def _params(n_axes):
    return pltpu.CompilerParams(
        dimension_semantics=("arbitrary",) * n_axes, vmem_limit_bytes=VMEM_LIMIT)


def _const_spec(shape):
    nd = len(shape)
    return pl.BlockSpec(shape, lambda *_: (0,) * nd)


def _rms_bf16(x, g):
    ms = jnp.mean(x * x, axis=-1, keepdims=True)
    return (x * lax.rsqrt(ms + NORM_EPS) * g).astype(BF16)


def _dot(a, b):
    return jnp.dot(a, b, preferred_element_type=F32)


def _split_dot(a, b_exact, terms):
    acc = None
    rem = a
    for _ in range(terms):
        part = rem.astype(BF16)
        d = _dot(part, b_exact)
        acc = d if acc is None else acc + d
        rem = rem - part.astype(F32)
    return acc


def _uv_kernel(x_ref, g_ref, w_ref, vg_ref, vb_ref, u_out, v_out):
    h = _rms_bf16(x_ref[...], g_ref[...])
    uv = _dot(h, w_ref[...])
    act = 0.5 * uv * (1.0 + lax.erf(uv * SQRT_HALF))
    u = act[:, :GMLP_WIDTH]
    v = act[:, GMLP_WIDTH:]
    mu = jnp.mean(v, axis=-1, keepdims=True)
    vc = v - mu
    var = jnp.mean(vc * vc, axis=-1, keepdims=True)
    vn = vc * lax.rsqrt(var + NORM_EPS) * vg_ref[...] + vb_ref[...]
    u_out[...] = u.astype(BF16)
    v_out[...] = vn.astype(BF16)


def _zgd_kernel(x_ref, g_ref, wz_ref, wg_ref, wdt_ref, bg_ref, dtb_ref, zs_out, gate_out, dt_out):
    h = _rms_bf16(x_ref[...], g_ref[...])
    z = _dot(h, wz_ref[...])
    zs_out[...] = (z * jax.nn.sigmoid(z)).astype(BF16)
    gl = _dot(h, wg_ref[...]) + bg_ref[...]
    gate_out[...] = jax.nn.sigmoid(gl).astype(BF16)
    dtr = _dot(h, wdt_ref[...]) + dtb_ref[...]
    dt_out[...] = jnp.maximum(dtr, 0.0) + jnp.log1p(jnp.exp(-jnp.abs(dtr)))


def _xbc_kernel(x_ref, g_ref, w_ref, cw_ref, cb_ref, out_ref, buf_ref, *, tm):
    @pl.when(pl.program_id(1) == 0)
    def _():
        buf_ref[0:8, :] = jnp.zeros((8, CONV_DIM), F32)

    h = _rms_bf16(x_ref[...], g_ref[...])
    buf_ref[8:8 + tm, :] = _dot(h, w_ref[...])
    acc = cb_ref[...] + cw_ref[CONV_WIDTH - 1:CONV_WIDTH, :] * buf_ref[8:8 + tm, :]
    for k in range(1, CONV_WIDTH):
        acc = acc + cw_ref[CONV_WIDTH - 1 - k:CONV_WIDTH - k, :] * buf_ref[8 - k:8 - k + tm, :]
    out_ref[...] = (acc * jax.nn.sigmoid(acc)).astype(BF16)
    buf_ref[0:8, :] = buf_ref[tm:tm + 8, :]


def _mix_kernel(u_ref, v_ref, zs_ref, xbc_ref, dt_ref, gate_ref, x_ref,
                wsp_ref, bsp_ref, a_ref, dskip_ref, ng_ref, e64_ref,
                wpa_ref, wpb_ref, wo_ref, out_ref, state_ref, ya_ref, yb_ref, *, nch):
    @pl.when(pl.program_id(1) == 0)
    def _():
        state_ref[...] = jnp.zeros(state_ref.shape, F32)

    row = lax.broadcasted_iota(jnp.int32, (CHUNK, CHUNK), 0)
    col = lax.broadcasted_iota(jnp.int32, (CHUNK, CHUNK), 1)
    causal = row >= col
    tril = jnp.where(causal, 1.0, 0.0).astype(BF16)

    for g in range(GMLP_GROUPS):
        cs_ = slice(g * GMLP_GROUP_DIM, (g + 1) * GMLP_GROUP_DIM)
        w = jnp.where(causal, wsp_ref[g], jnp.zeros((CHUNK, CHUNK), BF16))
        vcat = jnp.concatenate(
            [v_ref[c * CHUNK:(c + 1) * CHUNK, cs_] for c in range(nch)], axis=1)
        s = _dot(w, vcat)
        for c in range(nch):
            rs = slice(c * CHUNK, (c + 1) * CHUNK)
            sc = s[:, c * CHUNK:(c + 1) * CHUNK] + bsp_ref[g]
            ya_ref[rs, cs_] = (u_ref[rs, cs_].astype(F32) * sc).astype(BF16)

    for c in range(nch):
        rs = slice(c * CHUNK, (c + 1) * CHUNK)
        dtc = dt_ref[rs, :]
        da = dtc * a_ref[...]
        cs = _cumsum_rows(tril, da)
        cs_t = cs.T
        dt_t = dtc.T
        e_cs = jnp.exp(cs)
        w_state = jnp.exp(cs[CHUNK - 1:CHUNK, :] - cs) * dtc
        e_cs_x = _split_dot(e_cs, e64_ref[...], 2)
        w_state_x = _split_dot(w_state, e64_ref[...], 2)
        xs = xbc_ref[rs, 0:D_INNER]
        xs_f = xs.astype(F32)
        xw = (xs_f * w_state_x).astype(BF16)
        for g in range(N_GROUPS):
            gs = slice(g * GROUP_INNER, (g + 1) * GROUP_INNER)
            bm = xbc_ref[rs, D_INNER + g * D_STATE:D_INNER + (g + 1) * D_STATE]
            cm = xbc_ref[rs, D_INNER + BC_DIM + g * D_STATE:D_INNER + BC_DIM + (g + 1) * D_STATE]
            cb = lax.dot_general(cm, bm, (((1,), (1,)), ((), ())), preferred_element_type=F32)
            h_prev = state_ref[g]
            y_off = _dot(cm, h_prev.astype(BF16)) * e_cs_x[:, gs]
            st = lax.dot_general(bm, xw[:, gs], (((0,), (0,)), ((), ())),
                                 preferred_element_type=F32)
            state_ref[g] = h_prev * e_cs_x[CHUNK - 1:CHUNK, gs] + st
            y_heads = []
            for r in range(HEADS_PER_GROUP):
                hh = g * HEADS_PER_GROUP + r
                seg = cs[:, hh:hh + 1] - cs_t[hh:hh + 1, :]
                lmat = jnp.exp(jnp.where(causal, seg, -jnp.inf))
                m = (cb * lmat * dt_t[hh:hh + 1, :]).astype(BF16)
                y_heads.append(_dot(m, xs[:, hh * HEAD_DIM:(hh + 1) * HEAD_DIM]))
            y = jnp.concatenate(y_heads, axis=1) + y_off + dskip_ref[:, gs] * xs_f[:, gs]
            yg = y * zs_ref[rs, gs].astype(F32)
            ms = jnp.mean(yg * yg, axis=-1, keepdims=True)
            yb_ref[rs, gs] = (yg * lax.rsqrt(ms + NORM_EPS) * ng_ref[:, gs]).astype(BF16)

    pa = _dot(ya_ref[...], wpa_ref[...])
    pb = _dot(yb_ref[...], wpb_ref[...])
    merged = (gate_ref[:, 0:D_MODEL].astype(F32) * pa
              + gate_ref[:, D_MODEL:2 * D_MODEL].astype(F32) * pb)
    out_ref[...] = x_ref[...] + _dot(merged.astype(BF16), wo_ref[...])


def _cumsum_rows(tril, da):
    return _split_dot_left(tril, da, 3)


def _split_dot_left(a_exact, b, terms):
    acc = None
    rem = b
    for _ in range(terms):
        part = rem.astype(BF16)
        d = _dot(a_exact, part)
        acc = d if acc is None else acc + d
        rem = rem - part.astype(F32)
    return acc


def _mlp_kernel(x_ref, g_ref, wu_ref, wd_ref, gf_ref, out_ref):
    x = x_ref[...]
    h = _rms_bf16(x, g_ref[...])
    a = jnp.maximum(_dot(h, wu_ref[...]), 0.0)
    y = x + _dot((a * a).astype(BF16), wd_ref[...])
    ms = jnp.mean(y * y, axis=-1, keepdims=True)
    out_ref[...] = y * lax.rsqrt(ms + NORM_EPS) * gf_ref[...]


def _layer(x, norm_mix_g, w_in, conv_w, conv_b, dt_bias, a_log, d_skip, ssm_norm_g,
           v_norm_g, v_norm_b, w_spatial, b_spatial, b_gates, w_proj_a, w_proj_b, w_out,
           norm_mlp_g, w_mlp_up, w_mlp_down, final_g):
    bsz, seqlen, _ = x.shape
    tm = 512
    tl = 256
    nch = tl // CHUNK
    n_tm = seqlen // tm
    n_tl = seqlen // tl

    o_uv = 2 * GMLP_WIDTH
    o_z = o_uv + D_INNER
    o_xbc = o_z + CONV_DIM
    o_dt = o_xbc + N_HEADS
    w_uv = w_in[:, :o_uv].astype(BF16)
    w_z = w_in[:, o_uv:o_z].astype(BF16)
    w_xbc = w_in[:, o_z:o_xbc].astype(BF16)
    w_dt = jnp.pad(w_in[:, o_xbc:o_dt], ((0, 0), (0, DT_PAD - N_HEADS))).astype(BF16)
    w_g = w_in[:, o_dt:].astype(BF16)

    row = lambda v: v.reshape(1, -1).astype(F32)
    g_mix = row(norm_mix_g)
    x_spec = lambda t: pl.BlockSpec((None, t, D_MODEL), lambda b, j: (b, j, 0))
    tok_spec = lambda t, n: pl.BlockSpec((None, t, n), lambda b, j: (b, j, 0))
    tok_shape = lambda n, dt: jax.ShapeDtypeStruct((bsz, seqlen, n), dt)

    u_act, v_ln = pl.pallas_call(
        _uv_kernel,
        grid=(bsz, n_tm),
        in_specs=[x_spec(tm), _const_spec((1, D_MODEL)), _const_spec((D_MODEL, o_uv)),
                  _const_spec((1, GMLP_WIDTH)), _const_spec((1, GMLP_WIDTH))],
        out_specs=[tok_spec(tm, GMLP_WIDTH), tok_spec(tm, GMLP_WIDTH)],
        out_shape=[tok_shape(GMLP_WIDTH, BF16), tok_shape(GMLP_WIDTH, BF16)],
        compiler_params=_params(2),
        name="uv_proj",
    )(x, g_mix, w_uv, row(v_norm_g), row(v_norm_b))

    dt_bias_p = jnp.pad(dt_bias.astype(F32), (0, DT_PAD - N_HEADS)).reshape(1, DT_PAD)
    zs, gates, dt = pl.pallas_call(
        _zgd_kernel,
        grid=(bsz, n_tm),
        in_specs=[x_spec(tm), _const_spec((1, D_MODEL)), _const_spec((D_MODEL, D_INNER)),
                  _const_spec((D_MODEL, 2 * D_MODEL)), _const_spec((D_MODEL, DT_PAD)),
                  _const_spec((1, 2 * D_MODEL)), _const_spec((1, DT_PAD))],
        out_specs=[tok_spec(tm, D_INNER), tok_spec(tm, 2 * D_MODEL), tok_spec(tm, DT_PAD)],
        out_shape=[tok_shape(D_INNER, BF16), tok_shape(2 * D_MODEL, BF16), tok_shape(DT_PAD, F32)],
        compiler_params=_params(2),
        name="zgd_proj",
    )(x, g_mix, w_z, w_g, w_dt, row(b_gates), dt_bias_p)

    tmx = 256
    xbc = pl.pallas_call(
        functools.partial(_xbc_kernel, tm=tmx),
        grid=(bsz, seqlen // tmx),
        in_specs=[x_spec(tmx), _const_spec((1, D_MODEL)), _const_spec((D_MODEL, CONV_DIM)),
                  _const_spec((CONV_WIDTH, CONV_DIM)), _const_spec((1, CONV_DIM))],
        out_specs=tok_spec(tmx, CONV_DIM),
        out_shape=tok_shape(CONV_DIM, BF16),
        scratch_shapes=[pltpu.VMEM((tmx + 8, CONV_DIM), F32)],
        compiler_params=_params(2),
        name="xbc_conv",
    )(x, g_mix, w_xbc, conv_w.reshape(CONV_WIDTH, CONV_DIM).astype(F32), row(conv_b))

    a_row = jnp.pad(-jnp.exp(a_log.astype(F32)), (0, DT_PAD - N_HEADS)).reshape(1, DT_PAD)
    dskip_x = jnp.repeat(d_skip.astype(F32), HEAD_DIM).reshape(1, D_INNER)
    bsp_x = jnp.broadcast_to(b_spatial.astype(F32)[:, :, None], (GMLP_GROUPS, CHUNK, CHUNK))
    head_of_lane = jnp.arange(D_INNER, dtype=jnp.int32) // HEAD_DIM
    e64 = (jnp.arange(DT_PAD, dtype=jnp.int32)[:, None] == head_of_lane[None, :]).astype(BF16)

    x1 = pl.pallas_call(
        functools.partial(_mix_kernel, nch=nch),
        grid=(bsz, n_tl),
        in_specs=[tok_spec(tl, GMLP_WIDTH), tok_spec(tl, GMLP_WIDTH), tok_spec(tl, D_INNER),
                  tok_spec(tl, CONV_DIM), tok_spec(tl, DT_PAD), tok_spec(tl, 2 * D_MODEL),
                  x_spec(tl),
                  _const_spec((GMLP_GROUPS, CHUNK, CHUNK)), _const_spec((GMLP_GROUPS, CHUNK, CHUNK)),
                  _const_spec((1, DT_PAD)), _const_spec((1, D_INNER)), _const_spec((1, D_INNER)),
                  _const_spec((DT_PAD, D_INNER)),
                  _const_spec((GMLP_WIDTH, D_MODEL)), _const_spec((D_INNER, D_MODEL)),
                  _const_spec((D_MODEL, D_MODEL))],
        out_specs=x_spec(tl),
        out_shape=tok_shape(D_MODEL, F32),
        scratch_shapes=[pltpu.VMEM((N_GROUPS, D_STATE, GROUP_INNER), F32),
                        pltpu.VMEM((tl, GMLP_WIDTH), BF16),
                        pltpu.VMEM((tl, D_INNER), BF16)],
        compiler_params=_params(2),
        name="mix_merge",
    )(u_act, v_ln, zs, xbc, dt, gates, x,
      w_spatial.astype(BF16), bsp_x, a_row, dskip_x, row(ssm_norm_g), e64,
      w_proj_a.astype(BF16), w_proj_b.astype(BF16), w_out.astype(BF16))

    return pl.pallas_call(
        _mlp_kernel,
        grid=(bsz, n_tm),
        in_specs=[x_spec(tm), _const_spec((1, D_MODEL)), _const_spec((D_MODEL, D_FF)),
                  _const_spec((D_FF, D_MODEL)), _const_spec((1, D_MODEL))],
        out_specs=x_spec(tm),
        out_shape=tok_shape(D_MODEL, F32),
        compiler_params=_params(2),
        name="mlp_final",
    )(x1, row(norm_mlp_g), w_mlp_up.astype(BF16), w_mlp_down.astype(BF16), row(final_g))


def kernel(x, norm_mix_g, w_in, conv_w, conv_b, dt_bias, a_log, d_skip, ssm_norm_g, v_norm_g,
           v_norm_b, w_spatial, b_spatial, b_gates, w_proj_a, w_proj_b, w_out, norm_mlp_g,
           w_mlp_up, w_mlp_down, norm_final_g):
    depth = w_in.shape[0]
    assert depth == 1, "final RMSNorm is fused into the (single) layer's MLP kernel"
    return _layer(x, norm_mix_g[0], w_in[0], conv_w[0], conv_b[0], dt_bias[0], a_log[0], d_skip[0],
                  ssm_norm_g[0], v_norm_g[0], v_norm_b[0], w_spatial[0], b_spatial[0], b_gates[0],
                  w_proj_a[0], w_proj_b[0], w_out[0], norm_mlp_g[0], w_mlp_up[0], w_mlp_down[0],
                  norm_final_g)
```

```python
import functools
import math

import jax
import jax.numpy as jnp
from jax import lax
from jax.experimental import pallas as pl
from jax.experimental.pallas import tpu as pltpu

F32 = jnp.float32
BF16 = jnp.bfloat16

D_MODEL = 1024
NORM_EPS = 1e-6
CHUNK = 128
GMLP_WIDTH = D_MODEL
GMLP_GROUPS = 8
GMLP_GROUP_DIM = GMLP_WIDTH // GMLP_GROUPS
D_INNER = 2 * D_MODEL
HEAD_DIM = 64
N_HEADS = D_INNER // HEAD_DIM
N_GROUPS = 8
HEADS_PER_GROUP = N_HEADS // N_GROUPS
D_STATE = 128
CONV_WIDTH = 4
BC_DIM = N_GROUPS * D_STATE
CONV_DIM = D_INNER + 2 * BC_DIM
GROUP_INNER = D_INNER // N_GROUPS
D_FF = 4 * D_MODEL
LANES = 128
SUBLANES = 8
PERM_STRIDE = CHUNK // SUBLANES
DT_PAD = LANES
SQRT_HALF = math.sqrt(0.5)

VMEM_LIMIT = 56 * 1024 * 1024


def _params(n_axes, flags=None):
    return pltpu.CompilerParams(
        dimension_semantics=("arbitrary",) * n_axes, vmem_limit_bytes=VMEM_LIMIT, flags=flags)


def _const_spec(shape):
    nd = len(shape)
    return pl.BlockSpec(shape, lambda *_: (0,) * nd)


def _rms_bf16(x, g):
    ms = jnp.mean(x * x, axis=-1, keepdims=True)
    return (x * lax.rsqrt(ms + NORM_EPS) * g).astype(BF16)


def _dot(a, b):
    return jnp.dot(a, b, preferred_element_type=F32)


def _uv_kernel(x_ref, g_ref, w_ref, vg_ref, vb_ref, u_out, v_out):
    h = _rms_bf16(x_ref[...], g_ref[...])
    uv = _dot(h, w_ref[...])
    act = 0.5 * uv * (1.0 + lax.erf(uv * SQRT_HALF))
    u = act[:, :GMLP_WIDTH]
    v = act[:, GMLP_WIDTH:]
    mu = jnp.mean(v, axis=-1, keepdims=True)
    vc = v - mu
    var = jnp.mean(vc * vc, axis=-1, keepdims=True)
    vn = vc * lax.rsqrt(var + NORM_EPS) * vg_ref[...] + vb_ref[...]
    u_out[...] = u.astype(BF16)
    v_out[...] = vn.astype(BF16)


def _zgd_kernel(x_ref, g_ref, wz_ref, wg_ref, wdt_ref, bg_ref, dtb_ref, zs_out, gate_out, dt_out):
    h = _rms_bf16(x_ref[...], g_ref[...])
    z = _dot(h, wz_ref[...])
    zs_out[...] = (z * jax.nn.sigmoid(z)).astype(BF16)
    gl = _dot(h, wg_ref[...]) + bg_ref[...]
    gate_out[...] = jax.nn.sigmoid(gl).astype(BF16)
    dtr = _dot(h, wdt_ref[...]) + dtb_ref[...]
    dt_out[...] = jnp.maximum(dtr, 0.0) + jnp.log1p(jnp.exp(-jnp.abs(dtr)))


def _xbc_kernel(x_ref, g_ref, w_ref, cw_ref, cb_ref, perm_ref, unperm_ref, out_ref, tail_ref, *, tm):
    n_tail = (CONV_WIDTH - 1) * SUBLANES
    @pl.when(pl.program_id(1) == 0)
    def _():
        tail_ref[...] = jnp.zeros(tail_ref.shape, F32)

    h = _rms_bf16(x_ref[...], g_ref[...])
    hp = jnp.concatenate(
        [_dot(perm_ref[...], h[c * CHUNK:(c + 1) * CHUNK, :]).astype(BF16) for c in range(tm // CHUNK)],
        axis=0)
    p_all = _dot(hp, w_ref[...])
    first_sublane = lax.broadcasted_iota(jnp.int32, (SUBLANES, CONV_DIM), 0) == 0
    for c in range(tm // CHUNK):
        p = p_all[c * CHUNK:(c + 1) * CHUNK, :]
        prev = tail_ref[...] if c == 0 else p_all[c * CHUNK - n_tail:c * CHUNK, :]
        wrapped = []
        for j in range(CONV_WIDTH - 1):
            own = pltpu.roll(p[CHUNK - n_tail + j * SUBLANES:CHUNK - n_tail + (j + 1) * SUBLANES, :], 1, axis=0)
            old = pltpu.roll(prev[j * SUBLANES:(j + 1) * SUBLANES, :], 1, axis=0)
            wrapped.append(jnp.where(first_sublane, old, own))
        acc = cb_ref[...] + cw_ref[CONV_WIDTH - 1:CONV_WIDTH, :] * p
        for k in range(1, CONV_WIDTH):
            shifted = jnp.concatenate(wrapped[CONV_WIDTH - 1 - k:] + [p[0:CHUNK - k * SUBLANES, :]], axis=0)
            acc = acc + cw_ref[CONV_WIDTH - 1 - k:CONV_WIDTH - k, :] * shifted
        act = (acc * jax.nn.sigmoid(acc)).astype(BF16)
        out_ref[c * CHUNK:(c + 1) * CHUNK, :] = _dot(unperm_ref[...], act).astype(BF16)
    tail_ref[...] = p_all[tm - n_tail:tm, :]


def _mix_kernel(u_ref, v_ref, zs_ref, xbc_ref, dt_ref, gate_ref, x_ref,
                wsp_ref, bsp_ref, a_ref, dskip_ref, ng_ref, e64_ref, hmask_ref,
                wpa_ref, wpb_ref, wo_ref, out_ref, state_ref, ya_ref, yb_ref, *, nch):
    @pl.when(pl.program_id(1) == 0)
    def _():
        state_ref[...] = jnp.zeros(state_ref.shape, F32)

    row = lax.broadcasted_iota(jnp.int32, (CHUNK, CHUNK), 0)
    col = lax.broadcasted_iota(jnp.int32, (CHUNK, CHUNK), 1)
    causal = row >= col
    tril = jnp.where(causal, 1.0, 0.0).astype(BF16)

    for g in range(GMLP_GROUPS):
        cs_ = slice(g * GMLP_GROUP_DIM, (g + 1) * GMLP_GROUP_DIM)
        w = jnp.where(causal, wsp_ref[g], jnp.zeros((CHUNK, CHUNK), BF16))
        vcat = jnp.concatenate(
            [v_ref[c * CHUNK:(c + 1) * CHUNK, cs_] for c in range(nch)], axis=1)
        s = _dot(w, vcat)
        for c in range(nch):
            rs = slice(c * CHUNK, (c + 1) * CHUNK)
            sc = s[:, c * CHUNK:(c + 1) * CHUNK] + bsp_ref[g]
            ya_ref[rs, cs_] = (u_ref[rs, cs_].astype(F32) * sc).astype(BF16)

    da_all = jnp.concatenate(
        [dt_ref[c * CHUNK:(c + 1) * CHUNK, :] * a_ref[...] for c in range(nch)], axis=1)
    cs_all = _cumsum_rows(tril, da_all)
    head_masks = [hmask_ref[r] > 0 for r in range(HEADS_PER_GROUP)]
    zero_x = jnp.zeros((CHUNK, GROUP_INNER), BF16)
    for c in range(nch):
        rs = slice(c * CHUNK, (c + 1) * CHUNK)
        dtc = dt_ref[rs, :]
        cs = cs_all[:, c * DT_PAD:(c + 1) * DT_PAD]
        cs_t = cs.T
        dt_t = dtc.T
        e_cs = jnp.exp(cs)
        w_state = jnp.exp(cs[CHUNK - 1:CHUNK, :] - cs) * dtc
        ex = _dot(jnp.concatenate([e_cs, w_state], axis=0).astype(BF16), e64_ref[...])
        e_cs_x = ex[0:CHUNK, :]
        w_state_x = ex[CHUNK:2 * CHUNK, :]
        xs = xbc_ref[rs, 0:D_INNER]
        xs_f = xs.astype(F32)
        xw = (xs_f * w_state_x).astype(BF16)
        for g in range(N_GROUPS):
            gs = slice(g * GROUP_INNER, (g + 1) * GROUP_INNER)
            bm = xbc_ref[rs, D_INNER + g * D_STATE:D_INNER + (g + 1) * D_STATE]
            cm = xbc_ref[rs, D_INNER + BC_DIM + g * D_STATE:D_INNER + BC_DIM + (g + 1) * D_STATE]
            cb = lax.dot_general(cm, bm, (((1,), (1,)), ((), ())), preferred_element_type=F32)
            h_prev = state_ref[g]
            y_off = _dot(cm, h_prev.astype(BF16)) * e_cs_x[:, gs]
            st = lax.dot_general(bm, xw[:, gs], (((0,), (0,)), ((), ())),
                                 preferred_element_type=F32)
            state_ref[g] = h_prev * e_cs_x[CHUNK - 1:CHUNK, gs] + st
            m_heads = []
            for r in range(HEADS_PER_GROUP):
                hh = g * HEADS_PER_GROUP + r
                seg = cs[:, hh:hh + 1] - cs_t[hh:hh + 1, :]
                lmat = jnp.exp(jnp.where(causal, seg, -jnp.inf))
                m_heads.append((cb * lmat * dt_t[hh:hh + 1, :]).astype(BF16))
            xs_g = xs[:, gs]
            x_bd = jnp.concatenate(
                [jnp.where(head_masks[r], xs_g, zero_x) for r in range(HEADS_PER_GROUP)], axis=0)
            y_diag = _dot(jnp.concatenate(m_heads, axis=1), x_bd)
            y = y_diag + y_off + dskip_ref[:, gs] * xs_f[:, gs]
            yg = y * zs_ref[rs, gs].astype(F32)
            ms = jnp.mean(yg * yg, axis=-1, keepdims=True)
            yb_ref[rs, gs] = (yg * lax.rsqrt(ms + NORM_EPS) * ng_ref[:, gs]).astype(BF16)

    pa = _dot(ya_ref[...], wpa_ref[...])
    pb = _dot(yb_ref[...], wpb_ref[...])
    merged = (gate_ref[:, 0:D_MODEL].astype(F32) * pa
              + gate_ref[:, D_MODEL:2 * D_MODEL].astype(F32) * pb)
    out_ref[...] = x_ref[...] + _dot(merged.astype(BF16), wo_ref[...])


def _cumsum_rows(tril, da):
    return _split_dot_left(tril, da, 3)


def _split_dot_left(a_exact, b, terms):
    acc = None
    rem = b
    for _ in range(terms):
        part = rem.astype(BF16)
        d = _dot(a_exact, part)
        acc = d if acc is None else acc + d
        rem = rem - part.astype(F32)
    return acc


def _mlp_kernel(x_ref, g_ref, wu_ref, wd_ref, gf_ref, out_ref):
    x = x_ref[...]
    h = _rms_bf16(x, g_ref[...])
    a = jnp.maximum(_dot(h, wu_ref[...]), 0.0)
    y = x + _dot((a * a).astype(BF16), wd_ref[...])
    ms = jnp.mean(y * y, axis=-1, keepdims=True)
    out_ref[...] = y * lax.rsqrt(ms + NORM_EPS) * gf_ref[...]


def _layer(x, norm_mix_g, w_in, conv_w, conv_b, dt_bias, a_log, d_skip, ssm_norm_g,
           v_norm_g, v_norm_b, w_spatial, b_spatial, b_gates, w_proj_a, w_proj_b, w_out,
           norm_mlp_g, w_mlp_up, w_mlp_down, final_g):
    bsz, seqlen, _ = x.shape
    tm = 512
    tl = 256
    nch = tl // CHUNK
    n_tm = seqlen // tm
    n_tl = seqlen // tl

    o_uv = 2 * GMLP_WIDTH
    o_z = o_uv + D_INNER
    o_xbc = o_z + CONV_DIM
    o_dt = o_xbc + N_HEADS
    w_uv = w_in[:, :o_uv].astype(BF16)
    w_z = w_in[:, o_uv:o_z].astype(BF16)
    w_xbc = w_in[:, o_z:o_xbc].astype(BF16)
    w_dt = jnp.pad(w_in[:, o_xbc:o_dt], ((0, 0), (0, DT_PAD - N_HEADS))).astype(BF16)
    w_g = w_in[:, o_dt:].astype(BF16)

    row = lambda v: v.reshape(1, -1).astype(F32)
    g_mix = row(norm_mix_g)
    x_spec = lambda t: pl.BlockSpec((None, t, D_MODEL), lambda b, j: (b, j, 0))
    tok_spec = lambda t, n: pl.BlockSpec((None, t, n), lambda b, j: (b, j, 0))
    tok_shape = lambda n, dt: jax.ShapeDtypeStruct((bsz, seqlen, n), dt)

    u_act, v_ln = pl.pallas_call(
        _uv_kernel,
        grid=(bsz, n_tm),
        in_specs=[x_spec(tm), _const_spec((1, D_MODEL)), _const_spec((D_MODEL, o_uv)),
                  _const_spec((1, GMLP_WIDTH)), _const_spec((1, GMLP_WIDTH))],
        out_specs=[tok_spec(tm, GMLP_WIDTH), tok_spec(tm, GMLP_WIDTH)],
        out_shape=[tok_shape(GMLP_WIDTH, BF16), tok_shape(GMLP_WIDTH, BF16)],
        compiler_params=_params(2),
        name="uv_proj",
    )(x, g_mix, w_uv, row(v_norm_g), row(v_norm_b))

    dt_bias_p = jnp.pad(dt_bias.astype(F32), (0, DT_PAD - N_HEADS)).reshape(1, DT_PAD)
    zs, gates, dt = pl.pallas_call(
        _zgd_kernel,
        grid=(bsz, n_tm),
        in_specs=[x_spec(tm), _const_spec((1, D_MODEL)), _const_spec((D_MODEL, D_INNER)),
                  _const_spec((D_MODEL, 2 * D_MODEL)), _const_spec((D_MODEL, DT_PAD)),
                  _const_spec((1, 2 * D_MODEL)), _const_spec((1, DT_PAD))],
        out_specs=[tok_spec(tm, D_INNER), tok_spec(tm, 2 * D_MODEL), tok_spec(tm, DT_PAD)],
        out_shape=[tok_shape(D_INNER, BF16), tok_shape(2 * D_MODEL, BF16), tok_shape(DT_PAD, F32)],
        compiler_params=_params(2),
        name="zgd_proj",
    )(x, g_mix, w_z, w_g, w_dt, row(b_gates), dt_bias_p)

    tmx = 256
    pos = jnp.arange(CHUNK, dtype=jnp.int32)
    row_of_pos = (pos % PERM_STRIDE) * SUBLANES + pos // PERM_STRIDE
    unperm = (row_of_pos[:, None] == pos[None, :]).astype(BF16)
    xbc = pl.pallas_call(
        functools.partial(_xbc_kernel, tm=tmx),
        grid=(bsz, seqlen // tmx),
        in_specs=[x_spec(tmx), _const_spec((1, D_MODEL)), _const_spec((D_MODEL, CONV_DIM)),
                  _const_spec((CONV_WIDTH, CONV_DIM)), _const_spec((1, CONV_DIM)),
                  _const_spec((CHUNK, CHUNK)), _const_spec((CHUNK, CHUNK))],
        out_specs=tok_spec(tmx, CONV_DIM),
        out_shape=tok_shape(CONV_DIM, BF16),
        scratch_shapes=[pltpu.VMEM(((CONV_WIDTH - 1) * SUBLANES, CONV_DIM), F32)],
        compiler_params=_params(2),
        name="xbc_conv",
    )(x, g_mix, w_xbc, conv_w.reshape(CONV_WIDTH, CONV_DIM).astype(F32), row(conv_b),
      unperm.T, unperm)

    a_row = jnp.pad(-jnp.exp(a_log.astype(F32)), (0, DT_PAD - N_HEADS)).reshape(1, DT_PAD)
    dskip_x = jnp.repeat(d_skip.astype(F32), HEAD_DIM).reshape(1, D_INNER)
    bsp_x = jnp.broadcast_to(b_spatial.astype(F32)[:, :, None], (GMLP_GROUPS, CHUNK, CHUNK))
    head_of_lane = jnp.arange(D_INNER, dtype=jnp.int32) // HEAD_DIM
    e64 = (jnp.arange(DT_PAD, dtype=jnp.int32)[:, None] == head_of_lane[None, :]).astype(BF16)
    hmask = jnp.broadcast_to(
        (jnp.arange(HEADS_PER_GROUP, dtype=jnp.int32)[:, None, None]
         == head_of_lane[None, None, :GROUP_INNER]).astype(BF16),
        (HEADS_PER_GROUP, CHUNK, GROUP_INNER))

    x1 = pl.pallas_call(
        functools.partial(_mix_kernel, nch=nch),
        grid=(bsz, n_tl),
        in_specs=[tok_spec(tl, GMLP_WIDTH), tok_spec(tl, GMLP_WIDTH), tok_spec(tl, D_INNER),
                  tok_spec(tl, CONV_DIM), tok_spec(tl, DT_PAD), tok_spec(tl, 2 * D_MODEL),
                  x_spec(tl),
                  _const_spec((GMLP_GROUPS, CHUNK, CHUNK)), _const_spec((GMLP_GROUPS, CHUNK, CHUNK)),
                  _const_spec((1, DT_PAD)), _const_spec((1, D_INNER)), _const_spec((1, D_INNER)),
                  _const_spec((DT_PAD, D_INNER)), _const_spec((HEADS_PER_GROUP, CHUNK, GROUP_INNER)),
                  _const_spec((GMLP_WIDTH, D_MODEL)), _const_spec((D_INNER, D_MODEL)),
                  _const_spec((D_MODEL, D_MODEL))],
        out_specs=x_spec(tl),
        out_shape=tok_shape(D_MODEL, F32),
        scratch_shapes=[pltpu.VMEM((N_GROUPS, D_STATE, GROUP_INNER), F32),
                        pltpu.VMEM((tl, GMLP_WIDTH), BF16),
                        pltpu.VMEM((tl, D_INNER), BF16)],
        compiler_params=_params(2),
        name="mix_merge",
    )(u_act, v_ln, zs, xbc, dt, gates, x,
      w_spatial.astype(BF16), bsp_x, a_row, dskip_x, row(ssm_norm_g), e64, hmask,
      w_proj_a.astype(BF16), w_proj_b.astype(BF16), w_out.astype(BF16))

    return pl.pallas_call(
        _mlp_kernel,
        grid=(bsz, n_tm),
        in_specs=[x_spec(tm), _const_spec((1, D_MODEL)), _const_spec((D_MODEL, D_FF)),
                  _const_spec((D_FF, D_MODEL)), _const_spec((1, D_MODEL))],
        out_specs=x_spec(tm),
        out_shape=tok_shape(D_MODEL, F32),
        compiler_params=_params(2),
        name="mlp_final",
    )(x1, row(norm_mlp_g), w_mlp_up.astype(BF16), w_mlp_down.astype(BF16), row(final_g))


def kernel(x, norm_mix_g, w_in, conv_w, conv_b, dt_bias, a_log, d_skip, ssm_norm_g, v_norm_g,
           v_norm_b, w_spatial, b_spatial, b_gates, w_proj_a, w_proj_b, w_out, norm_mlp_g,
           w_mlp_up, w_mlp_down, norm_final_g):
    depth = w_in.shape[0]
    assert depth == 1, "final RMSNorm is fused into the (single) layer's MLP kernel"
    return _layer(x, norm_mix_g[0], w_in[0], conv_w[0], conv_b[0], dt_bias[0], a_log[0], d_skip[0],
                  ssm_norm_g[0], v_norm_g[0], v_norm_b[0], w_spatial[0], b_spatial[0], b_gates[0],
                  w_proj_a[0], w_proj_b[0], w_out[0], norm_mlp_g[0], w_mlp_up[0], w_mlp_down[0],
                  norm_final_g)
```

```python
import functools
import math

import jax
import jax.numpy as jnp
from jax import lax
from jax.experimental import pallas as pl
from jax.experimental.pallas import tpu as pltpu

F32 = jnp.float32
BF16 = jnp.bfloat16

D_MODEL = 1024
NORM_EPS = 1e-6
CHUNK = 128
GMLP_WIDTH = D_MODEL
GMLP_GROUPS = 8
GMLP_GROUP_DIM = GMLP_WIDTH // GMLP_GROUPS
D_INNER = 2 * D_MODEL
HEAD_DIM = 64
N_HEADS = D_INNER // HEAD_DIM
N_GROUPS = 8
HEADS_PER_GROUP = N_HEADS // N_GROUPS
D_STATE = 128
CONV_WIDTH = 4
BC_DIM = N_GROUPS * D_STATE
CONV_DIM = D_INNER + 2 * BC_DIM
GROUP_INNER = D_INNER // N_GROUPS
D_FF = 4 * D_MODEL
LANES = 128
SUBLANES = 8
PERM_STRIDE = CHUNK // SUBLANES
ROW_BLOCK = 256
DT_PAD = LANES
SQRT_HALF = math.sqrt(0.5)
LOG2E = 1.0 / math.log(2.0)

VMEM_LIMIT = 56 * 1024 * 1024


def _params(n_axes, flags=None):
    return pltpu.CompilerParams(
        dimension_semantics=("arbitrary",) * n_axes, vmem_limit_bytes=VMEM_LIMIT, flags=flags)


def _const_spec(shape):
    nd = len(shape)
    return pl.BlockSpec(shape, lambda *_: (0,) * nd)


def _rms_bf16(x, g):
    ms = jnp.mean(x * x, axis=-1, keepdims=True)
    return (x * lax.rsqrt(ms + NORM_EPS) * g).astype(BF16)


def _dot(a, b):
    return jnp.dot(a, b, preferred_element_type=F32)


def _row_blocks(tm):
    return [slice(r, r + ROW_BLOCK) for r in range(0, tm, ROW_BLOCK)]


O_UV = 2 * GMLP_WIDTH
O_Z = O_UV + D_INNER
O_XBC = O_Z + CONV_DIM
O_DT = O_XBC + N_HEADS


def _inproj_kernel(x_ref, g_ref, w_ref, wg_ref, wdt_ref, vg_ref, vb_ref, bg_ref, dtb_ref,
                   cw_ref, cb_ref, perm_ref, unperm_ref,
                   u_out, v_out, zs_out, gate_out, dt_out, xbc_out, tail_ref):
    tm = x_ref.shape[0]
    n_tail = (CONV_WIDTH - 1) * SUBLANES
    @pl.when(pl.program_id(1) == 0)
    def _():
        tail_ref[...] = jnp.zeros(tail_ref.shape, F32)

    first_sublane = lax.broadcasted_iota(jnp.int32, (SUBLANES, CONV_DIM), 0) == 0
    prev = tail_ref[...]
    for blk in _row_blocks(tm):
        h = _rms_bf16(x_ref[blk, :], g_ref[...])

        uv = _dot(h, w_ref[:, 0:O_UV])
        act = 0.5 * uv * (1.0 + lax.erf(uv * SQRT_HALF))
        v = act[:, GMLP_WIDTH:]
        mu = jnp.mean(v, axis=-1, keepdims=True)
        vc = v - mu
        var = jnp.mean(vc * vc, axis=-1, keepdims=True)
        vn = vc * lax.rsqrt(var + NORM_EPS) * vg_ref[...] + vb_ref[...]
        u_out[blk, :] = act[:, :GMLP_WIDTH].astype(BF16)
        v_out[blk, :] = vn.astype(BF16)

        z = _dot(h, w_ref[:, O_UV:O_Z])
        zs_out[blk, :] = (z * jax.nn.sigmoid(z)).astype(BF16)
        gl = _dot(h, wg_ref[...]) + bg_ref[...]
        gate_out[blk, :] = jax.nn.sigmoid(gl).astype(BF16)
        dtr = _dot(h, wdt_ref[...]) + dtb_ref[...]
        dt_out[blk, :] = jnp.maximum(dtr, 0.0) + jnp.log1p(jnp.exp(-jnp.abs(dtr)))

        w_xbc = w_ref[:, O_Z:O_XBC]
        hp = jnp.concatenate(
            [_dot(perm_ref[...], h[c * CHUNK:(c + 1) * CHUNK, :]).astype(BF16)
             for c in range(ROW_BLOCK // CHUNK)], axis=0)
        p_all = _dot(hp, w_xbc)
        for c in range(ROW_BLOCK // CHUNK):
            p = p_all[c * CHUNK:(c + 1) * CHUNK, :]
            wrapped = []
            for j in range(CONV_WIDTH - 1):
                own = pltpu.roll(p[CHUNK - n_tail + j * SUBLANES:CHUNK - n_tail + (j + 1) * SUBLANES, :], 1, axis=0)
                old = pltpu.roll(prev[j * SUBLANES:(j + 1) * SUBLANES, :], 1, axis=0)
                wrapped.append(jnp.where(first_sublane, old, own))
            acc = cb_ref[...] + cw_ref[CONV_WIDTH - 1:CONV_WIDTH, :] * p
            for k in range(1, CONV_WIDTH):
                shifted = jnp.concatenate(wrapped[CONV_WIDTH - 1 - k:] + [p[0:CHUNK - k * SUBLANES, :]], axis=0)
                acc = acc + cw_ref[CONV_WIDTH - 1 - k:CONV_WIDTH - k, :] * shifted
            act = (acc * jax.nn.sigmoid(acc)).astype(BF16)
            xbc_out[blk.start + c * CHUNK:blk.start + (c + 1) * CHUNK, :] = (
                _dot(unperm_ref[...], act).astype(BF16))
            prev = p[CHUNK - n_tail:CHUNK, :]
    tail_ref[...] = prev


def _mix_kernel(u_ref, v_ref, zs_ref, xbc_ref, dt_ref, gate_ref, x_ref,
                wsp_ref, bsp_ref, a_ref, dskip_ref, ng_ref, e64_ref,
                wpa_ref, wpb_ref, wo_ref, out_ref, state_ref, ya_ref, yb_ref, *, nch):
    @pl.when(pl.program_id(1) == 0)
    def _():
        state_ref[...] = jnp.zeros(state_ref.shape, F32)

    row = lax.broadcasted_iota(jnp.int32, (CHUNK, CHUNK), 0)
    col = lax.broadcasted_iota(jnp.int32, (CHUNK, CHUNK), 1)
    causal = row >= col
    tril = jnp.where(causal, 1.0, 0.0).astype(BF16)

    for g in range(GMLP_GROUPS):
        cs_ = slice(g * GMLP_GROUP_DIM, (g + 1) * GMLP_GROUP_DIM)
        w = jnp.where(causal, wsp_ref[g], jnp.zeros((CHUNK, CHUNK), BF16))
        vcat = jnp.concatenate(
            [v_ref[c * CHUNK:(c + 1) * CHUNK, cs_] for c in range(nch)], axis=1)
        s = _dot(w, vcat)
        for c in range(nch):
            rs = slice(c * CHUNK, (c + 1) * CHUNK)
            sc = s[:, c * CHUNK:(c + 1) * CHUNK] + bsp_ref[g]
            ya_ref[rs, cs_] = (u_ref[rs, cs_].astype(F32) * sc).astype(BF16)

    da_all = jnp.concatenate(
        [dt_ref[c * CHUNK:(c + 1) * CHUNK, :] * a_ref[...] for c in range(nch)], axis=1)
    cs_all = _cumsum_rows(tril, da_all)
    for c in range(nch):
        rs = slice(c * CHUNK, (c + 1) * CHUNK)
        dtc = dt_ref[rs, :]
        cs = cs_all[:, c * DT_PAD:(c + 1) * DT_PAD]
        cs2 = cs * LOG2E
        col_t = (cs2 - jnp.log(dtc) * LOG2E).T
        e_cs = jnp.exp(cs)
        w_state = jnp.exp(cs[CHUNK - 1:CHUNK, :] - cs) * dtc
        ex = _dot(jnp.concatenate([e_cs, w_state], axis=0).astype(BF16), e64_ref[...])
        e_cs_x = ex[0:CHUNK, :]
        w_state_x = ex[CHUNK:2 * CHUNK, :]
        xs = xbc_ref[rs, 0:D_INNER]
        xs_f = xs.astype(F32)
        xw = (xs_f * w_state_x).astype(BF16)
        for g in range(N_GROUPS):
            gs = slice(g * GROUP_INNER, (g + 1) * GROUP_INNER)
            bm = xbc_ref[rs, D_INNER + g * D_STATE:D_INNER + (g + 1) * D_STATE]
            cm = xbc_ref[rs, D_INNER + BC_DIM + g * D_STATE:D_INNER + BC_DIM + (g + 1) * D_STATE]
            cb = lax.dot_general(cm, bm, (((1,), (1,)), ((), ())), preferred_element_type=F32)
            h_prev = state_ref[g]
            y_off = _dot(cm, h_prev.astype(BF16)) * e_cs_x[:, gs]
            st = lax.dot_general(bm, xw[:, gs], (((0,), (0,)), ((), ())),
                                 preferred_element_type=F32)
            state_ref[g] = h_prev * e_cs_x[CHUNK - 1:CHUNK, gs] + st
            m_heads = []
            for r in range(HEADS_PER_GROUP):
                hh = g * HEADS_PER_GROUP + r
                seg2 = cs2[:, hh:hh + 1] - col_t[hh:hh + 1, :]
                lmat_dt = jnp.exp2(jnp.where(causal, seg2, -jnp.inf))
                m_heads.append((cb * lmat_dt).astype(BF16))
            x_bd = jnp.concatenate([_only_head(xs[:, gs], r) for r in range(HEADS_PER_GROUP)], axis=0)
            y_diag = _dot(jnp.concatenate(m_heads, axis=1), x_bd)
            y = y_diag + y_off + dskip_ref[:, gs] * xs_f[:, gs]
            yg = y * zs_ref[rs, gs].astype(F32)
            ms = jnp.mean(yg * yg, axis=-1, keepdims=True)
            yb_ref[rs, gs] = (yg * lax.rsqrt(ms + NORM_EPS) * ng_ref[:, gs]).astype(BF16)

    pa = _dot(ya_ref[...], wpa_ref[...])
    pb = _dot(yb_ref[...], wpb_ref[...])
    merged = (gate_ref[:, 0:D_MODEL].astype(F32) * pa
              + gate_ref[:, D_MODEL:2 * D_MODEL].astype(F32) * pb)
    out_ref[...] = x_ref[...] + _dot(merged.astype(BF16), wo_ref[...])


def _only_head(xs_g, r):
    parts = []
    if r > 0:
        parts.append(jnp.zeros((xs_g.shape[0], r * HEAD_DIM), xs_g.dtype))
    parts.append(xs_g[:, r * HEAD_DIM:(r + 1) * HEAD_DIM])
    if r < HEADS_PER_GROUP - 1:
        parts.append(jnp.zeros((xs_g.shape[0], (HEADS_PER_GROUP - 1 - r) * HEAD_DIM), xs_g.dtype))
    return jnp.concatenate(parts, axis=1)


def _cumsum_rows(tril, da):
    return _split_dot_left(tril, da, 3)


def _split_dot_left(a_exact, b, terms):
    acc = None
    rem = b
    for _ in range(terms):
        part = rem.astype(BF16)
        d = _dot(a_exact, part)
        acc = d if acc is None else acc + d
        rem = rem - part.astype(F32)
    return acc


def _mlp_kernel(x_ref, g_ref, wu_ref, wd_ref, gf_ref, out_ref):
    for rs in _row_blocks(x_ref.shape[0]):
        x = x_ref[rs, :]
        h = _rms_bf16(x, g_ref[...])
        a = jnp.maximum(_dot(h, wu_ref[...]), 0.0)
        y = x + _dot((a * a).astype(BF16), wd_ref[...])
        ms = jnp.mean(y * y, axis=-1, keepdims=True)
        out_ref[rs, :] = y * lax.rsqrt(ms + NORM_EPS) * gf_ref[...]


def _layer(x, norm_mix_g, w_in, conv_w, conv_b, dt_bias, a_log, d_skip, ssm_norm_g,
           v_norm_g, v_norm_b, w_spatial, b_spatial, b_gates, w_proj_a, w_proj_b, w_out,
           norm_mlp_g, w_mlp_up, w_mlp_down, final_g):
    bsz, seqlen, _ = x.shape
    tm = 512
    tl = 256
    nch = tl // CHUNK
    n_tm = seqlen // tm
    n_tl = seqlen // tl

    w_in_b = w_in.astype(BF16)
    w_dt = jnp.pad(w_in[:, O_XBC:O_DT], ((0, 0), (0, DT_PAD - N_HEADS))).astype(BF16)
    w_g = w_in[:, O_DT:].astype(BF16)

    row = lambda v: v.reshape(1, -1).astype(F32)
    x_spec = lambda t: pl.BlockSpec((None, t, D_MODEL), lambda b, j: (b, j, 0))
    tok_spec = lambda t, n: pl.BlockSpec((None, t, n), lambda b, j: (b, j, 0))
    tok_shape = lambda n, dt: jax.ShapeDtypeStruct((bsz, seqlen, n), dt)

    dt_bias_p = jnp.pad(dt_bias.astype(F32), (0, DT_PAD - N_HEADS)).reshape(1, DT_PAD)
    pos = jnp.arange(CHUNK, dtype=jnp.int32)
    row_of_pos = (pos % PERM_STRIDE) * SUBLANES + pos // PERM_STRIDE
    unperm = (row_of_pos[:, None] == pos[None, :]).astype(BF16)
    tmi = 256
    u_act, v_ln, zs, gates, dt, xbc = pl.pallas_call(
        _inproj_kernel,
        grid=(bsz, seqlen // tmi),
        in_specs=[x_spec(tmi), _const_spec((1, D_MODEL)),
                  pl.BlockSpec((D_MODEL, O_XBC), lambda b, j: (0, 0)),
                  _const_spec((D_MODEL, 2 * D_MODEL)), _const_spec((D_MODEL, DT_PAD)),
                  _const_spec((1, GMLP_WIDTH)), _const_spec((1, GMLP_WIDTH)),
                  _const_spec((1, 2 * D_MODEL)), _const_spec((1, DT_PAD)),
                  _const_spec((CONV_WIDTH, CONV_DIM)), _const_spec((1, CONV_DIM)),
                  _const_spec((CHUNK, CHUNK)), _const_spec((CHUNK, CHUNK))],
        out_specs=[tok_spec(tmi, GMLP_WIDTH), tok_spec(tmi, GMLP_WIDTH), tok_spec(tmi, D_INNER),
                   tok_spec(tmi, 2 * D_MODEL), tok_spec(tmi, DT_PAD), tok_spec(tmi, CONV_DIM)],
        out_shape=[tok_shape(GMLP_WIDTH, BF16), tok_shape(GMLP_WIDTH, BF16), tok_shape(D_INNER, BF16),
                   tok_shape(2 * D_MODEL, BF16), tok_shape(DT_PAD, F32), tok_shape(CONV_DIM, BF16)],
        scratch_shapes=[pltpu.VMEM(((CONV_WIDTH - 1) * SUBLANES, CONV_DIM), F32)],
        compiler_params=_params(2),
        name="in_proj",
    )(x, row(norm_mix_g), w_in_b, w_g, w_dt, row(v_norm_g), row(v_norm_b), row(b_gates), dt_bias_p,
      conv_w.reshape(CONV_WIDTH, CONV_DIM).astype(F32), row(conv_b), unperm.T, unperm)

    a_row = jnp.pad(-jnp.exp(a_log.astype(F32)), (0, DT_PAD - N_HEADS)).reshape(1, DT_PAD)
    dskip_x = jnp.repeat(d_skip.astype(F32), HEAD_DIM).reshape(1, D_INNER)
    bsp_x = jnp.broadcast_to(b_spatial.astype(F32)[:, :, None], (GMLP_GROUPS, CHUNK, CHUNK))
    head_of_lane = jnp.arange(D_INNER, dtype=jnp.int32) // HEAD_DIM
    e64 = (jnp.arange(DT_PAD, dtype=jnp.int32)[:, None] == head_of_lane[None, :]).astype(BF16)

    x1 = pl.pallas_call(
        functools.partial(_mix_kernel, nch=nch),
        grid=(bsz, n_tl),
        in_specs=[tok_spec(tl, GMLP_WIDTH), tok_spec(tl, GMLP_WIDTH), tok_spec(tl, D_INNER),
                  tok_spec(tl, CONV_DIM), tok_spec(tl, DT_PAD), tok_spec(tl, 2 * D_MODEL),
                  x_spec(tl),
                  _const_spec((GMLP_GROUPS, CHUNK, CHUNK)), _const_spec((GMLP_GROUPS, CHUNK, CHUNK)),
                  _const_spec((1, DT_PAD)), _const_spec((1, D_INNER)), _const_spec((1, D_INNER)),
                  _const_spec((DT_PAD, D_INNER)),
                  _const_spec((GMLP_WIDTH, D_MODEL)), _const_spec((D_INNER, D_MODEL)),
                  _const_spec((D_MODEL, D_MODEL))],
        out_specs=x_spec(tl),
        out_shape=tok_shape(D_MODEL, F32),
        scratch_shapes=[pltpu.VMEM((N_GROUPS, D_STATE, GROUP_INNER), F32),
                        pltpu.VMEM((tl, GMLP_WIDTH), BF16),
                        pltpu.VMEM((tl, D_INNER), BF16)],
        compiler_params=_params(2),
        name="mix_merge",
    )(u_act, v_ln, zs, xbc, dt, gates, x,
      w_spatial.astype(BF16), bsp_x, a_row, dskip_x, row(ssm_norm_g), e64,
      w_proj_a.astype(BF16), w_proj_b.astype(BF16), w_out.astype(BF16))

    return pl.pallas_call(
        _mlp_kernel,
        grid=(bsz, n_tm),
        in_specs=[x_spec(tm), _const_spec((1, D_MODEL)), _const_spec((D_MODEL, D_FF)),
                  _const_spec((D_FF, D_MODEL)), _const_spec((1, D_MODEL))],
        out_specs=x_spec(tm),
        out_shape=tok_shape(D_MODEL, F32),
        compiler_params=_params(2),
        name="mlp_final",
    )(x1, row(norm_mlp_g), w_mlp_up.astype(BF16), w_mlp_down.astype(BF16), row(final_g))


def kernel(x, norm_mix_g, w_in, conv_w, conv_b, dt_bias, a_log, d_skip, ssm_norm_g, v_norm_g,
           v_norm_b, w_spatial, b_spatial, b_gates, w_proj_a, w_proj_b, w_out, norm_mlp_g,
           w_mlp_up, w_mlp_down, norm_final_g):
    depth = w_in.shape[0]
    assert depth == 1, "final RMSNorm is fused into the (single) layer's MLP kernel"
    return _layer(x, norm_mix_g[0], w_in[0], conv_w[0], conv_b[0], dt_bias[0], a_log[0], d_skip[0],
                  ssm_norm_g[0], v_norm_g[0], v_norm_b[0], w_spatial[0], b_spatial[0], b_gates[0],
                  w_proj_a[0], w_proj_b[0], w_out[0], norm_mlp_g[0], w_mlp_up[0], w_mlp_down[0],
                  norm_final_g)
```

```python
import functools
import math

import jax
import jax.numpy as jnp
from jax import lax
from jax.experimental import pallas as pl
from jax.experimental.pallas import tpu as pltpu

F32 = jnp.float32
BF16 = jnp.bfloat16

D_MODEL = 1024
NORM_EPS = 1e-6
CHUNK = 128
GMLP_WIDTH = D_MODEL
GMLP_GROUPS = 8
GMLP_GROUP_DIM = GMLP_WIDTH // GMLP_GROUPS
D_INNER = 2 * D_MODEL
HEAD_DIM = 64
N_HEADS = D_INNER // HEAD_DIM
N_GROUPS = 8
HEADS_PER_GROUP = N_HEADS // N_GROUPS
D_STATE = 128
CONV_WIDTH = 4
BC_DIM = N_GROUPS * D_STATE
CONV_DIM = D_INNER + 2 * BC_DIM
GROUP_INNER = D_INNER // N_GROUPS
D_FF = 4 * D_MODEL
LANES = 128
SUBLANES = 8
PERM_STRIDE = CHUNK // SUBLANES
ROW_BLOCK = 256
DT_PAD = LANES
SQRT_HALF = math.sqrt(0.5)

VMEM_LIMIT = 56 * 1024 * 1024


def _params(n_axes, flags=None):
    return pltpu.CompilerParams(
        dimension_semantics=("arbitrary",) * n_axes, vmem_limit_bytes=VMEM_LIMIT, flags=flags)


def _const_spec(shape):
    nd = len(shape)
    return pl.BlockSpec(shape, lambda *_: (0,) * nd)


def _rms_bf16(x, g):
    ms = jnp.mean(x * x, axis=-1, keepdims=True)
    return (x * lax.rsqrt(ms + NORM_EPS) * g).astype(BF16)


def _dot(a, b):
    return jnp.dot(a, b, preferred_element_type=F32)


def _row_blocks(tm):
    return [slice(r, r + ROW_BLOCK) for r in range(0, tm, ROW_BLOCK)]


O_UV = 2 * GMLP_WIDTH
O_Z = O_UV + D_INNER
O_XBC = O_Z + CONV_DIM
O_DT = O_XBC + N_HEADS


def _inproj_kernel(x_ref, g_ref, w_ref, wg_ref, wdt_ref, vg_ref, vb_ref, bg_ref, dtb_ref,
                   cw_ref, cb_ref, perm_ref, unperm_ref,
                   u_out, v_out, zs_out, gate_out, dt_out, xbc_out, tail_ref):
    tm = x_ref.shape[0]
    n_tail = (CONV_WIDTH - 1) * SUBLANES
    @pl.when(pl.program_id(1) == 0)
    def _():
        tail_ref[...] = jnp.zeros(tail_ref.shape, F32)

    first_sublane = lax.broadcasted_iota(jnp.int32, (SUBLANES, CONV_DIM), 0) == 0
    prev = tail_ref[...]
    for blk in _row_blocks(tm):
        h = _rms_bf16(x_ref[blk, :], g_ref[...])

        uv = _dot(h, w_ref[:, 0:O_UV])
        act = 0.5 * uv * (1.0 + lax.erf(uv * SQRT_HALF))
        v = act[:, GMLP_WIDTH:]
        mu = jnp.mean(v, axis=-1, keepdims=True)
        vc = v - mu
        var = jnp.mean(vc * vc, axis=-1, keepdims=True)
        vn = vc * lax.rsqrt(var + NORM_EPS) * vg_ref[...] + vb_ref[...]
        u_out[blk, :] = act[:, :GMLP_WIDTH].astype(BF16)
        v_out[blk, :] = vn.astype(BF16)

        z = _dot(h, w_ref[:, O_UV:O_Z])
        zs_out[blk, :] = (z * jax.nn.sigmoid(z)).astype(BF16)
        gl = _dot(h, wg_ref[...]) + bg_ref[...]
        gate_out[blk, :] = jax.nn.sigmoid(gl).astype(BF16)
        dtr = _dot(h, wdt_ref[...]) + dtb_ref[...]
        dt_out[blk, :] = jnp.maximum(dtr, 0.0) + jnp.log1p(jnp.exp(-jnp.abs(dtr)))

        hp = jnp.concatenate(
            [_dot(perm_ref[...], h[c * CHUNK:(c + 1) * CHUNK, :]).astype(BF16)
             for c in range(ROW_BLOCK // CHUNK)], axis=0)
        p_all = _dot(hp, w_ref[:, O_Z:O_XBC])
        for c in range(ROW_BLOCK // CHUNK):
            p = p_all[c * CHUNK:(c + 1) * CHUNK, :]
            wrapped = []
            for j in range(CONV_WIDTH - 1):
                own = pltpu.roll(p[CHUNK - n_tail + j * SUBLANES:CHUNK - n_tail + (j + 1) * SUBLANES, :], 1, axis=0)
                old = pltpu.roll(prev[j * SUBLANES:(j + 1) * SUBLANES, :], 1, axis=0)
                wrapped.append(jnp.where(first_sublane, old, own))
            acc = cb_ref[...] + cw_ref[CONV_WIDTH - 1:CONV_WIDTH, :] * p
            for k in range(1, CONV_WIDTH):
                shifted = jnp.concatenate(wrapped[CONV_WIDTH - 1 - k:] + [p[0:CHUNK - k * SUBLANES, :]], axis=0)
                acc = acc + cw_ref[CONV_WIDTH - 1 - k:CONV_WIDTH - k, :] * shifted
            act = (acc * jax.nn.sigmoid(acc)).astype(BF16)
            xbc_out[blk.start + c * CHUNK:blk.start + (c + 1) * CHUNK, :] = (
                _dot(unperm_ref[...], act).astype(BF16))
            prev = p[CHUNK - n_tail:CHUNK, :]
    tail_ref[...] = prev


def _mix_kernel(u_ref, v_ref, zs_ref, xbc_ref, dt_ref, gate_ref, x_ref,
                wsp_ref, bsp_ref, a_ref, dskip_ref, ng_ref, e64_ref, hmask_ref,
                wpa_ref, wpb_ref, wo_ref, out_ref, state_ref, ya_ref, yb_ref, *, nch):
    @pl.when(pl.program_id(1) == 0)
    def _():
        state_ref[...] = jnp.zeros(state_ref.shape, F32)

    row = lax.broadcasted_iota(jnp.int32, (CHUNK, CHUNK), 0)
    col = lax.broadcasted_iota(jnp.int32, (CHUNK, CHUNK), 1)
    causal = row >= col
    tril = jnp.where(causal, 1.0, 0.0).astype(BF16)

    for g in range(GMLP_GROUPS):
        cs_ = slice(g * GMLP_GROUP_DIM, (g + 1) * GMLP_GROUP_DIM)
        w = jnp.where(causal, wsp_ref[g], jnp.zeros((CHUNK, CHUNK), BF16))
        vcat = jnp.concatenate(
            [v_ref[c * CHUNK:(c + 1) * CHUNK, cs_] for c in range(nch)], axis=1)
        s = _dot(w, vcat)
        for c in range(nch):
            rs = slice(c * CHUNK, (c + 1) * CHUNK)
            sc = s[:, c * CHUNK:(c + 1) * CHUNK] + bsp_ref[g]
            ya_ref[rs, cs_] = (u_ref[rs, cs_].astype(F32) * sc).astype(BF16)

    da_all = jnp.concatenate(
        [dt_ref[c * CHUNK:(c + 1) * CHUNK, :] * a_ref[...] for c in range(nch)], axis=1)
    cs_all = _cumsum_rows(tril, da_all)
    head_masks = [hmask_ref[r] > 0 for r in range(HEADS_PER_GROUP)]
    zero_x = jnp.zeros((CHUNK, GROUP_INNER), BF16)
    for c in range(nch):
        rs = slice(c * CHUNK, (c + 1) * CHUNK)
        dtc = dt_ref[rs, :]
        cs = cs_all[:, c * DT_PAD:(c + 1) * DT_PAD]
        cs_t = cs.T
        dt_t = dtc.T
        e_cs = jnp.exp(cs)
        w_state = jnp.exp(cs[CHUNK - 1:CHUNK, :] - cs) * dtc
        ex = _dot(jnp.concatenate([e_cs, w_state], axis=0).astype(BF16), e64_ref[...])
        e_cs_x = ex[0:CHUNK, :]
        w_state_x = ex[CHUNK:2 * CHUNK, :]
        xs = xbc_ref[rs, 0:D_INNER]
        xs_f = xs.astype(F32)
        xw = (xs_f * w_state_x).astype(BF16)
        for g in range(N_GROUPS):
            gs = slice(g * GROUP_INNER, (g + 1) * GROUP_INNER)
            bm = xbc_ref[rs, D_INNER + g * D_STATE:D_INNER + (g + 1) * D_STATE]
            cm = xbc_ref[rs, D_INNER + BC_DIM + g * D_STATE:D_INNER + BC_DIM + (g + 1) * D_STATE]
            cb = lax.dot_general(cm, bm, (((1,), (1,)), ((), ())), preferred_element_type=F32)
            h_prev = state_ref[g]
            y_off = _dot(cm, h_prev.astype(BF16)) * e_cs_x[:, gs]
            st = lax.dot_general(bm, xw[:, gs], (((0,), (0,)), ((), ())),
                                 preferred_element_type=F32)
            state_ref[g] = h_prev * e_cs_x[CHUNK - 1:CHUNK, gs] + st
            m_heads = []
            for r in range(HEADS_PER_GROUP):
                hh = g * HEADS_PER_GROUP + r
                seg = cs[:, hh:hh + 1] - cs_t[hh:hh + 1, :]
                lmat = jnp.exp(jnp.where(causal, seg, -jnp.inf))
                m_heads.append((cb * lmat * dt_t[hh:hh + 1, :]).astype(BF16))
            xs_g = xs[:, gs]
            x_bd = jnp.concatenate(
                [jnp.where(head_masks[r], xs_g, zero_x) for r in range(HEADS_PER_GROUP)], axis=0)
            y_diag = _dot(jnp.concatenate(m_heads, axis=1), x_bd)
            y = y_diag + y_off + dskip_ref[:, gs] * xs_f[:, gs]
            yg = y * zs_ref[rs, gs].astype(F32)
            ms = jnp.mean(yg * yg, axis=-1, keepdims=True)
            yb_ref[rs, gs] = (yg * lax.rsqrt(ms + NORM_EPS) * ng_ref[:, gs]).astype(BF16)

    pa = _dot(ya_ref[...], wpa_ref[...])
    pb = _dot(yb_ref[...], wpb_ref[...])
    merged = (gate_ref[:, 0:D_MODEL].astype(F32) * pa
              + gate_ref[:, D_MODEL:2 * D_MODEL].astype(F32) * pb)
    out_ref[...] = x_ref[...] + _dot(merged.astype(BF16), wo_ref[...])


def _cumsum_rows(tril, da):
    return _split_dot_left(tril, da, 3)


def _split_dot_left(a_exact, b, terms):
    acc = None
    rem = b
    for _ in range(terms):
        part = rem.astype(BF16)
        d = _dot(a_exact, part)
        acc = d if acc is None else acc + d
        rem = rem - part.astype(F32)
    return acc


def _mlp_kernel(x_ref, g_ref, wu_ref, wd_ref, gf_ref, out_ref):
    x = x_ref[...]
    h = _rms_bf16(x, g_ref[...])
    a = jnp.maximum(_dot(h, wu_ref[...]), 0.0)
    y = x + _dot((a * a).astype(BF16), wd_ref[...])
    ms = jnp.mean(y * y, axis=-1, keepdims=True)
    out_ref[...] = y * lax.rsqrt(ms + NORM_EPS) * gf_ref[...]


def _layer(x, norm_mix_g, w_in, conv_w, conv_b, dt_bias, a_log, d_skip, ssm_norm_g,
           v_norm_g, v_norm_b, w_spatial, b_spatial, b_gates, w_proj_a, w_proj_b, w_out,
           norm_mlp_g, w_mlp_up, w_mlp_down, final_g):
    bsz, seqlen, _ = x.shape
    tm = 512
    tl = 256
    nch = tl // CHUNK
    n_tm = seqlen // tm
    n_tl = seqlen // tl

    w_in_b = w_in.astype(BF16)
    w_dt = jnp.pad(w_in[:, O_XBC:O_DT], ((0, 0), (0, DT_PAD - N_HEADS))).astype(BF16)
    w_g = w_in[:, O_DT:].astype(BF16)

    row = lambda v: v.reshape(1, -1).astype(F32)
    x_spec = lambda t: pl.BlockSpec((None, t, D_MODEL), lambda b, j: (b, j, 0))
    tok_spec = lambda t, n: pl.BlockSpec((None, t, n), lambda b, j: (b, j, 0))
    tok_shape = lambda n, dt: jax.ShapeDtypeStruct((bsz, seqlen, n), dt)

    dt_bias_p = jnp.pad(dt_bias.astype(F32), (0, DT_PAD - N_HEADS)).reshape(1, DT_PAD)
    pos = jnp.arange(CHUNK, dtype=jnp.int32)
    row_of_pos = (pos % PERM_STRIDE) * SUBLANES + pos // PERM_STRIDE
    unperm = (row_of_pos[:, None] == pos[None, :]).astype(BF16)
    tmi = 256
    u_act, v_ln, zs, gates, dt, xbc = pl.pallas_call(
        _inproj_kernel,
        grid=(bsz, seqlen // tmi),
        in_specs=[x_spec(tmi), _const_spec((1, D_MODEL)),
                  pl.BlockSpec((D_MODEL, O_XBC), lambda b, j: (0, 0)),
                  _const_spec((D_MODEL, 2 * D_MODEL)), _const_spec((D_MODEL, DT_PAD)),
                  _const_spec((1, GMLP_WIDTH)), _const_spec((1, GMLP_WIDTH)),
                  _const_spec((1, 2 * D_MODEL)), _const_spec((1, DT_PAD)),
                  _const_spec((CONV_WIDTH, CONV_DIM)), _const_spec((1, CONV_DIM)),
                  _const_spec((CHUNK, CHUNK)), _const_spec((CHUNK, CHUNK))],
        out_specs=[tok_spec(tmi, GMLP_WIDTH), tok_spec(tmi, GMLP_WIDTH), tok_spec(tmi, D_INNER),
                   tok_spec(tmi, 2 * D_MODEL), tok_spec(tmi, DT_PAD), tok_spec(tmi, CONV_DIM)],
        out_shape=[tok_shape(GMLP_WIDTH, BF16), tok_shape(GMLP_WIDTH, BF16), tok_shape(D_INNER, BF16),
                   tok_shape(2 * D_MODEL, BF16), tok_shape(DT_PAD, F32), tok_shape(CONV_DIM, BF16)],
        scratch_shapes=[pltpu.VMEM(((CONV_WIDTH - 1) * SUBLANES, CONV_DIM), F32)],
        compiler_params=_params(2),
        name="in_proj",
    )(x, row(norm_mix_g), w_in_b, w_g, w_dt, row(v_norm_g), row(v_norm_b), row(b_gates), dt_bias_p,
      conv_w.reshape(CONV_WIDTH, CONV_DIM).astype(F32), row(conv_b), unperm.T, unperm)

    a_row = jnp.pad(-jnp.exp(a_log.astype(F32)), (0, DT_PAD - N_HEADS)).reshape(1, DT_PAD)
    dskip_x = jnp.repeat(d_skip.astype(F32), HEAD_DIM).reshape(1, D_INNER)
    bsp_x = jnp.broadcast_to(b_spatial.astype(F32)[:, :, None], (GMLP_GROUPS, CHUNK, CHUNK))
    head_of_lane = jnp.arange(D_INNER, dtype=jnp.int32) // HEAD_DIM
    e64 = (jnp.arange(DT_PAD, dtype=jnp.int32)[:, None] == head_of_lane[None, :]).astype(BF16)

    hmask = jnp.broadcast_to(
        (jnp.arange(HEADS_PER_GROUP, dtype=jnp.int32)[:, None, None]
         == head_of_lane[None, None, :GROUP_INNER]).astype(BF16),
        (HEADS_PER_GROUP, CHUNK, GROUP_INNER))

    x1 = pl.pallas_call(
        functools.partial(_mix_kernel, nch=nch),
        grid=(bsz, n_tl),
        in_specs=[tok_spec(tl, GMLP_WIDTH), tok_spec(tl, GMLP_WIDTH), tok_spec(tl, D_INNER),
                  tok_spec(tl, CONV_DIM), tok_spec(tl, DT_PAD), tok_spec(tl, 2 * D_MODEL),
                  x_spec(tl),
                  _const_spec((GMLP_GROUPS, CHUNK, CHUNK)), _const_spec((GMLP_GROUPS, CHUNK, CHUNK)),
                  _const_spec((1, DT_PAD)), _const_spec((1, D_INNER)), _const_spec((1, D_INNER)),
                  _const_spec((DT_PAD, D_INNER)), _const_spec((HEADS_PER_GROUP, CHUNK, GROUP_INNER)),
                  _const_spec((GMLP_WIDTH, D_MODEL)), _const_spec((D_INNER, D_MODEL)),
                  _const_spec((D_MODEL, D_MODEL))],
        out_specs=x_spec(tl),
        out_shape=tok_shape(D_MODEL, F32),
        scratch_shapes=[pltpu.VMEM((N_GROUPS, D_STATE, GROUP_INNER), F32),
                        pltpu.VMEM((tl, GMLP_WIDTH), BF16),
                        pltpu.VMEM((tl, D_INNER), BF16)],
        compiler_params=_params(2),
        name="mix_merge",
    )(u_act, v_ln, zs, xbc, dt, gates, x,
      w_spatial.astype(BF16), bsp_x, a_row, dskip_x, row(ssm_norm_g), e64, hmask,
      w_proj_a.astype(BF16), w_proj_b.astype(BF16), w_out.astype(BF16))

    return pl.pallas_call(
        _mlp_kernel,
        grid=(bsz, n_tm),
        in_specs=[x_spec(tm), _const_spec((1, D_MODEL)), _const_spec((D_MODEL, D_FF)),
                  _const_spec((D_FF, D_MODEL)), _const_spec((1, D_MODEL))],
        out_specs=x_spec(tm),
        out_shape=tok_shape(D_MODEL, F32),
        compiler_params=_params(2),
        name="mlp_final",
    )(x1, row(norm_mlp_g), w_mlp_up.astype(BF16), w_mlp_down.astype(BF16), row(final_g))


def kernel(x, norm_mix_g, w_in, conv_w, conv_b, dt_bias, a_log, d_skip, ssm_norm_g, v_norm_g,
           v_norm_b, w_spatial, b_spatial, b_gates, w_proj_a, w_proj_b, w_out, norm_mlp_g,
           w_mlp_up, w_mlp_down, norm_final_g):
    depth = w_in.shape[0]
    assert depth == 1, "final RMSNorm is fused into the (single) layer's MLP kernel"
    return _layer(x, norm_mix_g[0], w_in[0], conv_w[0], conv_b[0], dt_bias[0], a_log[0], d_skip[0],
                  ssm_norm_g[0], v_norm_g[0], v_norm_b[0], w_spatial[0], b_spatial[0], b_gates[0],
                  w_proj_a[0], w_proj_b[0], w_out[0], norm_mlp_g[0], w_mlp_up[0], w_mlp_down[0],
                  norm_final_g)
```
